```python
import math
import jax
import jax.numpy as jnp
from jax import lax
import numpy as np

D_MODEL = 1024
BATCH = 2
SEQ = 8192
DEPTH = 4
DEC_BATCH = 32
DEC_SEQ = 8
PAST_LEN = 8192
PAGE_SIZE = 128

HEAD_DIM = 64
MIX_WIDTH = D_MODEL
HALF_MIX = MIX_WIDTH // 2
H_A = HALF_MIX // HEAD_DIM
A_WIDTH = H_A * HEAD_DIM
DECAY_LORA = 64
AAA_LORA = 64
GATE_LORA = 128
N_A_COLS = 3 * A_WIDTH + DECAY_LORA + AAA_LORA + GATE_LORA
B_GROUPS = 8
B_WIDTH = HALF_MIX
B_CHUNK = 128
N_EVEN_IN = N_A_COLS + 2 * B_WIDTH
H_C = HALF_MIX // HEAD_DIM
C_WIDTH = H_C * HEAD_DIM
C_CONV_CH = 3 * C_WIDTH
CONV_W = 4
N_C_COLS = C_CONV_CH + C_WIDTH + 2 * H_C
GDN_CHUNK = 64
H_D = HALF_MIX // HEAD_DIM
D_WIDTH = H_D * HEAD_DIM
N_D_COLS = 3 * D_WIDTH + H_D
FOX_BLOCK = 128
N_ODD_IN = N_C_COLS + N_D_COLS
D_FF = 4 * D_MODEL
N_EVEN = (DEPTH + 1) // 2
N_ODD = DEPTH // 2
NORM_EPS = 1e-6
RWKV_GN_EPS = 64e-5

kernel_name = 'hybrid_rwkv7_gmlp_gdn_fox_decoder_step'


def rmsnorm(x, g):
    xf = x.astype(jnp.float32)
    y = xf * lax.rsqrt(jnp.mean(xf * xf, axis=-1, keepdims=True) + NORM_EPS)
    return (y * g.astype(jnp.float32)).astype(x.dtype)


def l2norm(x):
    xf = x.astype(jnp.float32)
    return xf * lax.rsqrt(jnp.sum(xf * xf, axis=-1, keepdims=True) + NORM_EPS)


def ada_terms(c, w_ada, b_ada):
    mod = jax.nn.silu(c.astype(jnp.float32)) @ w_ada + b_ada
    return jnp.split(mod[:, None, :], 6, axis=-1)


def modulate(x, g, shift, scale):
    return (rmsnorm(x, g) * (1.0 + scale) + shift).astype(x.dtype)


def residual(x, y, g, gate):
    return (x + gate * rmsnorm(y, g)).astype(x.dtype)


def channel_mlp(h, w_up, w_down):
    return jnp.square(jax.nn.relu(h @ w_up)) @ w_down


def gather_pages(pool, page_table):
    g = pool[page_table]
    return g.reshape((page_table.shape[0], page_table.shape[1] * pool.shape[1]) + pool.shape[2:])


def rwkv7_mix(p, shift_prev, wkv_prev, mu, w0, w_up, a0, a_up, g_up, k_k, k_a, r_k, gn_w, gn_b):
    f32 = jnp.float32
    bsz, L, _ = p.shape
    p = p.astype(f32)
    p_prev = jnp.concatenate([shift_prev[:, None].astype(f32), p[:, :-1]], axis=1)
    xm = p + (p_prev - p) * mu
    cuts = [A_WIDTH, 2 * A_WIDTH, 3 * A_WIDTH, 3 * A_WIDTH + DECAY_LORA, 3 * A_WIDTH + DECAY_LORA + AAA_LORA]
    r, k, v, wd, ad, gd = jnp.split(xm, cuts, axis=-1)
    w_log = -jax.nn.softplus(-(w0 + jnp.tanh(wd) @ w_up)) - 0.5
    decay = jnp.exp(-jnp.exp(w_log))
    a = jax.nn.sigmoid(a0 + ad @ a_up)
    g = jax.nn.sigmoid(gd) @ g_up
    hd = lambda t: t.reshape(bsz, L, H_A, HEAD_DIM)
    kk = l2norm(hd(k * k_k))
    k = k * (1.0 + (a - 1.0) * k_a)
    r, k, v, decay, a = hd(r), hd(k), hd(v), hd(decay), hd(a)

    def step(s, inp):
        r_t, w_t, k_t, v_t, kk_t, a_t = inp
        sa = jnp.einsum('bhvk,bhk->bhv', s, kk_t)
        s = (s * w_t[:, :, None, :] - sa[..., None] * (kk_t * a_t)[:, :, None, :]
             + v_t[..., None] * k_t[:, :, None, :])
        return s, jnp.einsum('bhvk,bhk->bhv', s, r_t)

    tm = lambda t: jnp.swapaxes(t, 0, 1)
    wkv, o = lax.scan(step, wkv_prev.astype(f32), (tm(r), tm(decay), tm(k), tm(v), tm(kk), tm(a)))
    o = tm(o)
    mean = jnp.mean(o, axis=-1, keepdims=True)
    var = jnp.mean(jnp.square(o - mean), axis=-1, keepdims=True)
    o = ((o - mean) * lax.rsqrt(var + RWKV_GN_EPS)).reshape(bsz, L, A_WIDTH) * gn_w + gn_b
    bonus = jnp.sum(r * k * r_k, axis=-1, keepdims=True) * v
    out = (o + bonus.reshape(bsz, L, A_WIDTH)) * g
    return out, wkv, p[:, -1]


def gmlp_mix(u, v, ln_g, ln_b, w_s, b_s):
    f32 = jnp.float32
    bsz, L, _ = v.shape
    vf = v.astype(f32)
    mean = jnp.mean(vf, axis=-1, keepdims=True)
    var = jnp.mean(jnp.square(vf - mean), axis=-1, keepdims=True)
    vn = (vf - mean) * lax.rsqrt(var + NORM_EPS) * ln_g + ln_b
    n_chunks = -(-L // B_CHUNK)
    pad = n_chunks * B_CHUNK - L
    vc = jnp.pad(vn, ((0, 0), (0, pad), (0, 0))).reshape(bsz, n_chunks, B_CHUNK, B_GROUPS, B_WIDTH // B_GROUPS)
    causal = jnp.tril(jnp.ones((B_CHUNK, B_CHUNK), f32))
    mixed = (jnp.einsum('gts,bnsgc->bntgc', w_s * causal, vc)
             + jnp.swapaxes(b_s, 0, 1)[None, None, :, :, None])
    mixed = mixed.reshape(bsz, n_chunks * B_CHUNK, B_WIDTH)[:, :L]
    return u.astype(f32) * mixed, vn


def causal_conv(x, prev, w):
    L = x.shape[1]
    xp = jnp.concatenate([prev.astype(x.dtype), x], axis=1)
    y = w[0] * xp[:, 0:L]
    for j in range(1, CONV_W):
        y = y + w[j] * xp[:, j:j + L]
    return y, xp[:, -(CONV_W - 1):]


def gated_delta_chunked(q, k, v, beta, g, s0):
    f32 = jnp.float32
    bsz, L, H, dk = q.shape
    dv = v.shape[-1]
    C = min(GDN_CHUNK, L)
    n = -(-L // C)
    pad = n * C - L

    def chunks(t):
        t = jnp.pad(t, ((0, 0), (0, pad)) + ((0, 0),) * (t.ndim - 2))
        t = t.reshape((bsz, n, C) + t.shape[2:])
        return jnp.moveaxis(jnp.moveaxis(t, 1, 0), 2, 3)

    q, k, v, beta, g = (chunks(t) for t in (q, k, v, beta, g))
    gc = jnp.cumsum(g, axis=-1)
    diff = gc[..., :, None] - gc[..., None, :]
    idx = jnp.arange(C)
    strict = idx[:, None] > idx[None, :]
    incl = idx[:, None] >= idx[None, :]
    dec_strict = jnp.exp(jnp.where(strict, diff, -jnp.inf))
    dec_incl = jnp.exp(jnp.where(incl, diff, -jnp.inf))
    a_mat = jnp.eye(C, dtype=f32) + beta[..., :, None] * jnp.einsum('nbhid,nbhjd->nbhij', k, k) * dec_strict
    rhs = jnp.concatenate([beta[..., None] * v, (beta * jnp.exp(gc))[..., None] * k], axis=-1)
    w = lax.linalg.triangular_solve(a_mat, rhs, left_side=True, lower=True, unit_diagonal=True)
    w_v, w_k = w[..., :dv], w[..., dv:]
    attn = jnp.einsum('nbhid,nbhjd->nbhij', q, k) * dec_incl
    q_dec = q * jnp.exp(gc)[..., None]
    k_end = k * jnp.exp(gc[..., -1:] - gc)[..., None]
    g_end = jnp.exp(gc[..., -1])

    def step(s, inp):
        w_v_c, w_k_c, attn_c, q_c, k_c, g_c = inp
        u = w_v_c - jnp.einsum('bhck,bhkv->bhcv', w_k_c, s)
        o = jnp.einsum('bhck,bhkv->bhcv', q_c, s) + jnp.einsum('bhij,bhjv->bhiv', attn_c, u)
        s = g_c[..., None, None] * s + jnp.einsum('bhck,bhcv->bhkv', k_c, u)
        return s, o

    s_fin, o = lax.scan(step, s0, (w_v, w_k, attn, q_dec, k_end, g_end))
    o = jnp.swapaxes(jnp.moveaxis(o, 0, 1), 2, 3).reshape(bsz, n * C, H, dv)[:, :L]
    return o, s_fin


def gdn_mix(cols, conv_prev, ssm_prev, conv_w, a_log, dt_bias, onorm_g):
    f32 = jnp.float32
    bsz, L, _ = cols.shape
    cols = cols.astype(f32)
    qkv, z, b_logit, a_logit = jnp.split(cols, [C_CONV_CH, C_CONV_CH + C_WIDTH, C_CONV_CH + C_WIDTH + H_C], axis=-1)
    conv, new_conv = causal_conv(qkv, conv_prev, conv_w)
    q, k, v = jnp.split(jax.nn.silu(conv), 3, axis=-1)
    hd = lambda t: t.reshape(bsz, L, H_C, HEAD_DIM)
    q = l2norm(hd(q)) * (HEAD_DIM ** -0.5)
    k = l2norm(hd(k))
    beta = jax.nn.sigmoid(b_logit)
    g = -jnp.exp(a_log.astype(f32)) * jax.nn.softplus(a_logit + dt_bias)
    o, ssm = gated_delta_chunked(q, k, hd(v), beta, g, ssm_prev.astype(f32))
    o = rmsnorm(o, onorm_g) * jax.nn.silu(hd(z))
    return o.reshape(bsz, L, C_WIDTH), ssm, new_conv


def fox_logits(q, fq, qpos, k, fk, kpos):
    s = jnp.einsum('bqhd,bkhd->bhqk', q, k).astype(jnp.float32) * (HEAD_DIM ** -0.5)
    s = s + jnp.swapaxes(fq, 1, 2)[..., :, None] - jnp.swapaxes(fk, 1, 2)[..., None, :]
    return jnp.where(kpos[None, None, None, :] <= qpos[None, None, :, None], s, -jnp.inf)


def fox_prompt(q, k, v, logf):
    bsz, L, H, dh = q.shape
    fcum = jnp.cumsum(logf, axis=1)
    kpos = jnp.arange(L)

    def block(i):
        start = i * FOX_BLOCK
        qb = lax.dynamic_slice_in_dim(q, start, FOX_BLOCK, axis=1)
        fb = lax.dynamic_slice_in_dim(fcum, start, FOX_BLOCK, axis=1)
        qpos = start + jnp.arange(FOX_BLOCK)
        p = jax.nn.softmax(fox_logits(qb, fb, qpos, k, fcum, kpos), axis=-1)
        return jnp.einsum('bhqk,bkhd->bqhd', p.astype(v.dtype), v)

    o = lax.map(block, jnp.arange(L // FOX_BLOCK))
    return jnp.moveaxis(o, 0, 1).reshape(bsz, L, H, dh)


def fox_sample(q, k, v, logf, k_past, v_past, logf_past):
    L = q.shape[1]
    P = k_past.shape[1]
    f_past = jnp.cumsum(logf_past.astype(jnp.float32), axis=1)
    f_new = f_past[:, -1:] + jnp.cumsum(logf, axis=1)
    qpos = P + jnp.arange(L)
    s = jnp.concatenate([fox_logits(q, f_new, qpos, k_past, f_past, jnp.arange(P)),
                         fox_logits(q, f_new, qpos, k, f_new, qpos)], axis=-1)
    p = jax.nn.softmax(s, axis=-1)
    return (jnp.einsum('bhqk,bkhd->bqhd', p[..., :P], v_past)
            + jnp.einsum('bhqk,bkhd->bqhd', p[..., P:], v))


def even_mixer(h, shift_prev, wkv_prev, w_in, w_out, mu, w0, w_up, a0, a_up, g_up, k_k, k_a, r_k,
               gn_w, gn_b, ln_g, ln_b, w_s, b_s):
    cols = h @ w_in
    o_a, wkv, shift = rwkv7_mix(cols[..., :N_A_COLS], shift_prev, wkv_prev, mu, w0, w_up, a0, a_up, g_up,
                                k_k, k_a, r_k, gn_w, gn_b)
    o_b, v_rows = gmlp_mix(cols[..., N_A_COLS:N_A_COLS + B_WIDTH], cols[..., N_A_COLS + B_WIDTH:],
                           ln_g, ln_b, w_s, b_s)
    y = jnp.concatenate([o_a, o_b], axis=-1).astype(h.dtype) @ w_out
    return y, wkv, shift, v_rows


def odd_mixer(h, conv_prev, ssm_prev, past, w_in, w_out, conv_w, a_log, dt_bias, onorm_g, f_bias):
    bsz, L, _ = h.shape
    cols = h @ w_in
    o_c, ssm, new_conv = gdn_mix(cols[..., :N_C_COLS], conv_prev, ssm_prev, conv_w, a_log, dt_bias, onorm_g)
    q, k, v, f_logit = jnp.split(cols[..., N_C_COLS:], [D_WIDTH, 2 * D_WIDTH, 3 * D_WIDTH], axis=-1)
    hd = lambda t: t.reshape(bsz, L, H_D, HEAD_DIM)
    q, k, v = hd(q), hd(k), hd(v)
    logf = jax.nn.log_sigmoid(f_logit.astype(jnp.float32) + f_bias)
    if past is None:
        o_d = fox_prompt(q, k, v, logf)
    else:
        o_d = fox_sample(q, k, v, logf, past[0], past[1], past[2])
    y = jnp.concatenate([o_c, o_d.reshape(bsz, L, D_WIDTH).astype(jnp.float32)], axis=-1).astype(h.dtype) @ w_out
    return y, ssm, new_conv, k, v, logf


def setup_inputs(seed: int = 0) -> dict:
    key = jax.random.key(seed)
    keys = iter(jax.random.split(key, 64))
    f32 = jnp.float32

    def nrm(shape, scale):
        return scale * jax.random.normal(next(keys), shape, f32)

    def uni(shape, lo, hi):
        return jax.random.uniform(next(keys), shape, f32, lo, hi)

    n_pages = PAST_LEN // PAGE_SIZE
    n_used = DEC_BATCH * n_pages
    n_pool = n_used + n_used // 4
    page_table = jax.random.permutation(next(keys), n_pool)[:n_used].reshape(DEC_BATCH, n_pages).astype(jnp.int32)
    dt = jnp.exp(uni((N_ODD, H_C), math.log(1e-3), math.log(1e-1)))
    dt_bias = dt + jnp.log(-jnp.expm1(-dt))
    return {
        'x_prompt': nrm((BATCH, SEQ, D_MODEL), 1.0),
        'x_sample': nrm((DEC_BATCH, DEC_SEQ, D_MODEL), 1.0),
        'state_a_wkv': nrm((N_EVEN, DEC_BATCH, H_A, HEAD_DIM, HEAD_DIM), 0.1),
        'state_a_shift': nrm((N_EVEN, DEC_BATCH, N_A_COLS), 1.0),
        'state_c_ssm': nrm((N_ODD, DEC_BATCH, H_C, HEAD_DIM, HEAD_DIM), 0.1),
        'state_c_conv': nrm((N_ODD, DEC_BATCH, CONV_W - 1, C_CONV_CH), 1.0),
        'cache_d_k': nrm((N_ODD, n_pool, PAGE_SIZE, H_D, HEAD_DIM), 1.0),
        'cache_d_v': nrm((N_ODD, n_pool, PAGE_SIZE, H_D, HEAD_DIM), 1.0),
        'cache_d_logf': jax.nn.log_sigmoid(2.0 + nrm((N_ODD, n_pool, PAGE_SIZE, H_D), 1.0)),
        'page_table': page_table,
        'c_prompt': nrm((BATCH, D_MODEL), 1.0),
        'c_sample': nrm((DEC_BATCH, D_MODEL), 1.0),
        'w_ada': nrm((DEPTH, D_MODEL, 6 * D_MODEL), 0.5 * D_MODEL ** -0.5),
        'b_ada': nrm((DEPTH, 6 * D_MODEL), 0.01),
        'norm_g': 1.0 + nrm((DEPTH, 4, D_MODEL), 0.05),
        'w_in_even': nrm((N_EVEN, D_MODEL, N_EVEN_IN), D_MODEL ** -0.5),
        'w_out_even': nrm((N_EVEN, MIX_WIDTH, D_MODEL), MIX_WIDTH ** -0.5),
        'a_mu': uni((N_EVEN, N_A_COLS), 0.0, 1.0),
        'a_w0': uni((N_EVEN, A_WIDTH), -5.0, 1.0),
        'a_w_up': nrm((N_EVEN, DECAY_LORA, A_WIDTH), 0.1),
        'a_a0': nrm((N_EVEN, A_WIDTH), 0.5),
        'a_a_up': nrm((N_EVEN, AAA_LORA, A_WIDTH), 0.1),
        'a_g_up': nrm((N_EVEN, GATE_LORA, A_WIDTH), GATE_LORA ** -0.5),
        'a_k_k': 0.85 + nrm((N_EVEN, A_WIDTH), 0.05),
        'a_k_a': 1.0 + nrm((N_EVEN, A_WIDTH), 0.05),
        'a_r_k': nrm((N_EVEN, H_A, HEAD_DIM), 0.1),
        'a_gn_w': 1.0 + nrm((N_EVEN, A_WIDTH), 0.05),
        'a_gn_b': nrm((N_EVEN, A_WIDTH), 0.01),
        'b_ln_g': 1.0 + nrm((N_EVEN, B_WIDTH), 0.05),
        'b_ln_b': nrm((N_EVEN, B_WIDTH), 0.01),
        'b_ws': nrm((N_EVEN, B_GROUPS, B_CHUNK, B_CHUNK), B_CHUNK ** -0.5),
        'b_bs': 1.0 + nrm((N_EVEN, B_GROUPS, B_CHUNK), 0.05),
        'w_in_odd': nrm((N_ODD, D_MODEL, N_ODD_IN), D_MODEL ** -0.5),
        'w_out_odd': nrm((N_ODD, MIX_WIDTH, D_MODEL), MIX_WIDTH ** -0.5),
        'c_conv_w': nrm((N_ODD, CONV_W, C_CONV_CH), CONV_W ** -0.5),
        'c_a_log': jnp.log(uni((N_ODD, H_C), 1.0, 16.0)),
        'c_dt_bias': dt_bias,
        'c_onorm_g': 1.0 + nrm((N_ODD, HEAD_DIM), 0.05),
        'd_f_bias': 2.0 + nrm((N_ODD, H_D), 0.5),
        'w_up': nrm((DEPTH, D_MODEL, D_FF), D_MODEL ** -0.5),
        'w_down': nrm((DEPTH, D_FF, D_MODEL), D_FF ** -0.5),
    }


def reference(x_prompt, x_sample, state_a_wkv, state_a_shift, state_c_ssm, state_c_conv, cache_d_k, cache_d_v,
              cache_d_logf, page_table, c_prompt, c_sample, w_ada, b_ada, norm_g, w_in_even, w_out_even, a_mu, a_w0,
              a_w_up, a_a0, a_a_up, a_g_up, a_k_k, a_k_a, a_r_k, a_gn_w, a_gn_b, b_ln_g, b_ln_b, b_ws, b_bs,
              w_in_odd, w_out_odd, c_conv_w, c_a_log, c_dt_bias, c_onorm_g, d_f_bias, w_up, w_down):
    f32 = jnp.float32
    bp = x_prompt.shape[0]
    zero_shift = jnp.zeros((bp, N_A_COLS), f32)
    zero_wkv = jnp.zeros((bp, H_A, HEAD_DIM, HEAD_DIM), f32)
    zero_ssm = jnp.zeros((bp, H_C, HEAD_DIM, HEAD_DIM), f32)
    zero_conv = jnp.zeros((bp, CONV_W - 1, C_CONV_CH), f32)
    wkv_p, shift_p, ssm_p, conv_p, dk_p, dv_p, dlf_p = [], [], [], [], [], [], []
    wkv_s, shift_s, bv_s, ssm_s, conv_s, dk_s, dv_s, dlf_s = [], [], [], [], [], [], [], []
    xp, xs = x_prompt, x_sample
    for l in range(DEPTH):
        i = l // 2
        sp1, cp1, gp1, sp2, cp2, gp2 = ada_terms(c_prompt, w_ada[l], b_ada[l])
        ss1, cs1, gs1, ss2, cs2, gs2 = ada_terms(c_sample, w_ada[l], b_ada[l])
        hp = modulate(xp, norm_g[l, 0], sp1, cp1)
        hs = modulate(xs, norm_g[l, 0], ss1, cs1)
        if l % 2 == 0:
            ew = (w_in_even[i], w_out_even[i], a_mu[i], a_w0[i], a_w_up[i], a_a0[i], a_a_up[i], a_g_up[i],
                  a_k_k[i], a_k_a[i], a_r_k[i], a_gn_w[i], a_gn_b[i], b_ln_g[i], b_ln_b[i], b_ws[i], b_bs[i])
            yp, wkv, shift, _ = even_mixer(hp, zero_shift, zero_wkv, *ew)
            wkv_p.append(wkv)
            shift_p.append(shift)
            ys, wkv, shift, v_rows = even_mixer(hs, state_a_shift[i], state_a_wkv[i], *ew)
            wkv_s.append(wkv)
            shift_s.append(shift)
            bv_s.append(v_rows)
        else:
            ow = (w_in_odd[i], w_out_odd[i], c_conv_w[i], c_a_log[i], c_dt_bias[i], c_onorm_g[i], d_f_bias[i])
            past = (gather_pages(cache_d_k[i], page_table), gather_pages(cache_d_v[i], page_table),
                    gather_pages(cache_d_logf[i], page_table))
            yp, ssm, conv, k_rows, v_rows, lf_rows = odd_mixer(hp, zero_conv, zero_ssm, None, *ow)
            ssm_p.append(ssm)
            conv_p.append(conv)
            dk_p.append(k_rows)
            dv_p.append(v_rows)
            dlf_p.append(lf_rows)
            ys, ssm, conv, k_rows, v_rows, lf_rows = odd_mixer(hs, state_c_conv[i], state_c_ssm[i], past, *ow)
            ssm_s.append(ssm)
            conv_s.append(conv)
            dk_s.append(k_rows)
            dv_s.append(v_rows)
            dlf_s.append(lf_rows)
        xp = residual(xp, yp, norm_g[l, 1], gp1)
        xs = residual(xs, ys, norm_g[l, 1], gs1)
        xp = residual(xp, channel_mlp(modulate(xp, norm_g[l, 2], sp2, cp2), w_up[l], w_down[l]), norm_g[l, 3], gp2)
        xs = residual(xs, channel_mlp(modulate(xs, norm_g[l, 2], ss2, cs2), w_up[l], w_down[l]), norm_g[l, 3], gs2)
    y_prompt, y_sample = xp, xs
    new_a_wkv_p, new_a_shift_p = jnp.stack(wkv_p), jnp.stack(shift_p)
    new_c_ssm_p, new_c_conv_p = jnp.stack(ssm_p), jnp.stack(conv_p)
    new_d_k_p, new_d_v_p, new_d_logf_p = jnp.stack(dk_p), jnp.stack(dv_p), jnp.stack(dlf_p)
    new_a_wkv_s, new_a_shift_s, new_b_v_s = jnp.stack(wkv_s), jnp.stack(shift_s), jnp.stack(bv_s)
    new_c_ssm_s, new_c_conv_s = jnp.stack(ssm_s), jnp.stack(conv_s)
    new_d_k_s, new_d_v_s, new_d_logf_s = jnp.stack(dk_s), jnp.stack(dv_s), jnp.stack(dlf_s)
    return (y_prompt, y_sample, new_a_wkv_p, new_a_shift_p, new_c_ssm_p, new_c_conv_p, new_d_k_p, new_d_v_p,
            new_d_logf_p, new_a_wkv_s, new_a_shift_s, new_b_v_s, new_c_ssm_s, new_c_conv_s, new_d_k_s, new_d_v_s,
            new_d_logf_s)
```

```python
import functools

import jax
import jax.numpy as jnp
from jax import lax
from jax.experimental import pallas as pl
from jax.experimental.pallas import tpu as pltpu

F32 = jnp.float32
BF16 = jnp.bfloat16

HEAD_DIM = 64
LANES = 128
NORM_EPS = 1e-6
RWKV_GN_EPS = 64e-5
DECAY_LORA = 64
AAA_LORA = 64
GATE_LORA = 128
CONV_W = 4
B_CHUNK = 128
GDN_CHUNK = 64
CARRY_ROWS = 8
VMEM_LIMIT = 56 * 1024 * 1024
NEG_INF = float("-inf")


def _mm(a, b):
    return jnp.dot(a.astype(BF16), b.astype(BF16), preferred_element_type=F32)


def _mm_nt(a, b):
    return lax.dot_general(a.astype(BF16), b.astype(BF16), (((1,), (1,)), ((), ())),
                           preferred_element_type=F32)


def _mm_tn(a, b):
    return lax.dot_general(a.astype(BF16), b.astype(BF16), (((0,), (0,)), ((), ())),
                           preferred_element_type=F32)


def _split3(x):
    hi = x.astype(BF16)
    r1 = x - hi.astype(F32)
    mid = r1.astype(BF16)
    lo = (r1 - mid.astype(F32)).astype(BF16)
    return hi, mid, lo


def _mm_exact_rhs(x, m01):
    hi, mid, lo = _split3(x)
    d = functools.partial(jnp.dot, preferred_element_type=F32)
    return d(hi, m01) + d(mid, m01) + d(lo, m01)


def _mm_exact_lhs(m01, x):
    hi, mid, lo = _split3(x)
    d = functools.partial(jnp.dot, preferred_element_type=F32)
    return d(m01, hi) + d(m01, mid) + d(m01, lo)


def _mm3(a, b):
    a_hi = a.astype(BF16)
    a_lo = (a - a_hi.astype(F32)).astype(BF16)
    b_hi = b.astype(BF16)
    b_lo = (b - b_hi.astype(F32)).astype(BF16)
    d = functools.partial(jnp.dot, preferred_element_type=F32)
    return d(a_hi, b_hi) + d(a_hi, b_lo) + d(a_lo, b_hi)


def _sigmoid(x):
    return 1.0 / (1.0 + jnp.exp(-x))


def _softplus(x):
    return jnp.maximum(x, 0.0) + jnp.log(1.0 + jnp.exp(-jnp.abs(x)))


def _silu(x):
    return x * _sigmoid(x)


def _rms(x, g):
    return x * lax.rsqrt(jnp.mean(x * x, axis=-1, keepdims=True) + NORM_EPS) * g


def _ones_where(mask):
    return jnp.where(mask, 1.0, 0.0).astype(BF16)


def _iota2(shape):
    return (lax.broadcasted_iota(jnp.int32, shape, 0), lax.broadcasted_iota(jnp.int32, shape, 1))


def _unit_lower_inv(a, eye):
    n = a.shape[0]
    inv = eye - a
    p = a
    k = 2
    while k < n:
        p = _mm3(p, p)
        inv = inv + _mm3(inv, p)
        k *= 2
    return inv


def _transpose_pad(x):
    rows = x.shape[0]
    if rows < LANES:
        x = jnp.concatenate([x, jnp.zeros((LANES - rows, LANES), F32)], axis=0)
    return x.T


def _ada_kernel(c_ref, w_ref, b_ref, o_ref):
    c = c_ref[...]
    o_ref[0] = _mm(_silu(c), w_ref[0]) + b_ref[0]


def _ada_call(c_all, w_ada, b_ada):
    depth, d, n = w_ada.shape
    rows = c_all.shape[0]
    tn = 1536
    return pl.pallas_call(
        _ada_kernel,
        out_shape=jax.ShapeDtypeStruct((depth, rows, n), F32),
        grid=(depth, n // tn),
        in_specs=[pl.BlockSpec((rows, d), lambda l, j: (0, 0)),
                  pl.BlockSpec((1, d, tn), lambda l, j: (l, 0, j)),
                  pl.BlockSpec((1, 1, tn), lambda l, j: (l, 0, j))],
        out_specs=pl.BlockSpec((1, rows, tn), lambda l, j: (l, 0, j)),
        compiler_params=pltpu.CompilerParams(dimension_semantics=("parallel", "parallel"),
                                             vmem_limit_bytes=VMEM_LIMIT),
        name="ada_terms",
    )(c_all, w_ada, b_ada.reshape(depth, 1, n))


def _inproj_kernel(x_ref, sc_ref, sh_ref, g_ref, w_ref, o_ref, xm_ref):
    @pl.when(pl.program_id(1) == 0)
    def _():
        xm_ref[...] = (_rms(x_ref[...], g_ref[...]) * (1.0 + sc_ref[0]) + sh_ref[0]).astype(BF16)

    o_ref[...] = jnp.dot(xm_ref[...], w_ref[...], preferred_element_type=F32)


def _inproj_call(x2, sc3, sh3, g, w_bf, tm, tn, tiles_per_mod):
    r, d = x2.shape
    n = w_bf.shape[1]
    rm = sc3.shape[1]
    mod_spec = pl.BlockSpec((1, rm, d), lambda i, j: (i // tiles_per_mod, 0, 0))
    return pl.pallas_call(
        _inproj_kernel,
        out_shape=jax.ShapeDtypeStruct((r, n), F32),
        grid=(r // tm, n // tn),
        in_specs=[pl.BlockSpec((tm, d), lambda i, j: (i, 0)), mod_spec, mod_spec,
                  pl.BlockSpec((1, d), lambda i, j: (0, 0)),
                  pl.BlockSpec((d, tn), lambda i, j: (0, j))],
        out_specs=pl.BlockSpec((tm, tn), lambda i, j: (i, j)),
        scratch_shapes=[pltpu.VMEM((tm, d), BF16)],
        compiler_params=pltpu.CompilerParams(dimension_semantics=("parallel", "arbitrary"),
                                             vmem_limit_bytes=VMEM_LIMIT),
        name="modulate_inproj",
    )(x2, sc3, sh3, g, w_bf)


def _outproj_kernel(oa_ref, ob_ref, x_ref, gate_ref, g_ref, wa_ref, wb_ref, o_ref):
    y = (jnp.dot(oa_ref[...].astype(BF16), wa_ref[...], preferred_element_type=F32)
         + jnp.dot(ob_ref[...].astype(BF16), wb_ref[...], preferred_element_type=F32))
    o_ref[...] = x_ref[...] + gate_ref[0] * _rms(y, g_ref[...])


def _outproj_call(oa, ob, x2, gate3, g, wa_bf, wb_bf, tm, tiles_per_mod):
    r, d = x2.shape
    ha = oa.shape[1]
    hb = ob.shape[1]
    rm = gate3.shape[1]
    return pl.pallas_call(
        _outproj_kernel,
        out_shape=jax.ShapeDtypeStruct((r, d), F32),
        grid=(r // tm,),
        in_specs=[pl.BlockSpec((tm, ha), lambda i: (i, 0)), pl.BlockSpec((tm, hb), lambda i: (i, 0)),
                  pl.BlockSpec((tm, d), lambda i: (i, 0)),
                  pl.BlockSpec((1, rm, d), lambda i: (i // tiles_per_mod, 0, 0)),
                  pl.BlockSpec((1, d), lambda i: (0, 0)),
                  pl.BlockSpec((ha, d), lambda i: (0, 0)), pl.BlockSpec((hb, d), lambda i: (0, 0))],
        out_specs=pl.BlockSpec((tm, d), lambda i: (i, 0)),
        compiler_params=pltpu.CompilerParams(dimension_semantics=("parallel",),
                                             vmem_limit_bytes=VMEM_LIMIT),
        name="outproj_residual",
    )(oa, ob, x2, gate3, g, wa_bf, wb_bf)


def _mlp_kernel(x_ref, sc_ref, sh_ref, gate_ref, g2_ref, g3_ref, wu_ref, wd_ref, o_ref, xm_ref, acc_ref):
    k = pl.program_id(1)

    @pl.when(k == 0)
    def _():
        xm_ref[...] = (_rms(x_ref[...], g2_ref[...]) * (1.0 + sc_ref[0]) + sh_ref[0]).astype(BF16)
        acc_ref[...] = jnp.zeros_like(acc_ref)

    h = jnp.dot(xm_ref[...], wu_ref[...], preferred_element_type=F32)
    h = jnp.square(jnp.maximum(h, 0.0))
    acc_ref[...] += jnp.dot(h.astype(BF16), wd_ref[...], preferred_element_type=F32)

    @pl.when(k == pl.num_programs(1) - 1)
    def _():
        o_ref[...] = x_ref[...] + gate_ref[0] * _rms(acc_ref[...], g3_ref[...])


def _mlp_call(x2, sc3, sh3, gate3, g2, g3, wu_bf, wd_bf, tm, tf, tiles_per_mod):
    r, d = x2.shape
    ff = wu_bf.shape[1]
    rm = sc3.shape[1]
    mod_spec = pl.BlockSpec((1, rm, d), lambda i, k: (i // tiles_per_mod, 0, 0))
    vec_spec = pl.BlockSpec((1, d), lambda i, k: (0, 0))
    return pl.pallas_call(
        _mlp_kernel,
        out_shape=jax.ShapeDtypeStruct((r, d), F32),
        grid=(r // tm, ff // tf),
        in_specs=[pl.BlockSpec((tm, d), lambda i, k: (i, 0)), mod_spec, mod_spec, mod_spec, vec_spec, vec_spec,
                  pl.BlockSpec((d, tf), lambda i, k: (0, k)), pl.BlockSpec((tf, d), lambda i, k: (k, 0))],
        out_specs=pl.BlockSpec((tm, d), lambda i, k: (i, 0)),
        scratch_shapes=[pltpu.VMEM((tm, d), BF16), pltpu.VMEM((tm, d), F32)],
        compiler_params=pltpu.CompilerParams(dimension_semantics=("parallel", "arbitrary"),
                                             vmem_limit_bytes=VMEM_LIMIT),
        name="relu2_mlp",
    )(x2, sc3, sh3, gate3, g2, g3, wu_bf, wd_bf)


def _rwkv_kernel(rkv_ref, lora_ref, sp_rkv_ref, sp_lora_ref, s0_ref, mu_rkv_ref, mu_lora_ref, w0_ref, wup_ref,
                 a0_ref, aup_ref, gup_ref, kk_ref, ka_ref, rk_ref, gnw_ref, gnb_ref, seg_ref,
                 o_ref, s_ref, so_rkv_ref, so_lora_ref, oh_ref, *, chunk, heads):
    C = chunk
    aw = heads * HEAD_DIM

    @pl.when(pl.program_id(1) == 0)
    def _():
        s_ref[...] = s0_ref[...]
        so_rkv_ref[...] = sp_rkv_ref[...]
        so_lora_ref[...] = sp_lora_ref[...]

    def shift_mix(p, prev_row, mu):
        row = lax.broadcasted_iota(jnp.int32, p.shape, 0)
        p_prev = jnp.where(row == 0, prev_row, pltpu.roll(p, 1, 0))
        return p + (p_prev - p) * mu

    p_rkv = rkv_ref[...]
    p_lora = lora_ref[...]
    xm = shift_mix(p_rkv, so_rkv_ref[0], mu_rkv_ref[...])
    xl = shift_mix(p_lora, so_lora_ref[0], mu_lora_ref[...])
    so_rkv_ref[0] = p_rkv[C - 1:C, :]
    so_lora_ref[0] = p_lora[C - 1:C, :]

    r = xm[:, :aw]
    k = xm[:, aw:2 * aw]
    v = xm[:, 2 * aw:]
    wd = xl[:, :DECAY_LORA]
    ad = xl[:, DECAY_LORA:DECAY_LORA + AAA_LORA]
    gd = xl[:, DECAY_LORA + AAA_LORA:]

    w_log = -_softplus(-(w0_ref[...] + _mm3(jnp.tanh(wd), wup_ref[...]))) - 0.5
    lw = -jnp.exp(w_log)
    a = _sigmoid(a0_ref[...] + _mm3(ad, aup_ref[...]))
    g = _mm3(_sigmoid(gd), gup_ref[...])
    seg = seg_ref[...]
    kx = k * kk_ref[...]
    kk = kx * lax.rsqrt(_mm_exact_rhs(kx * kx, seg) + NORM_EPS)
    kp = k * (1.0 + (a - 1.0) * ka_ref[...])
    b = kk * a

    ri, ci = _iota2((C, C))
    strict = ri > ci
    incl = ri >= ci
    eye = jnp.where(ri == ci, 1.0, 0.0)
    gcum = _mm_exact_lhs(_ones_where(incl), lw)
    g_last = gcum[C - 1:C, :]
    e_in = jnp.exp(gcum)
    e_out = jnp.exp(-gcum)
    e_end = jnp.exp(g_last - gcum)
    rd = r * e_in
    kkd = kk * jnp.exp(gcum - lw)
    bi = b * e_out
    ki = kp * e_out
    be = b * e_end
    ke = kp * e_end
    s_decay = jnp.exp(g_last)

    for h in range(heads):
        sl = slice(h * HEAD_DIM, (h + 1) * HEAD_DIM)
        lhs = jnp.concatenate([kkd[:, sl], rd[:, sl]], axis=0)
        rhs = jnp.concatenate([bi[:, sl], ki[:, sl]], axis=0)
        mm = _mm_nt(lhs, rhs)
        m1 = jnp.where(strict, mm[:C, :C], 0.0)
        m2 = jnp.where(strict, mm[:C, C:], 0.0)
        n1 = jnp.where(incl, mm[C:, :C], 0.0)
        n2 = jnp.where(incl, mm[C:, C:], 0.0)
        inv = _unit_lower_inv(m1, eye)
        s_prev = s_ref[0, h]
        x = _mm_nt(lhs, s_prev)
        vh = v[:, sl]
        sa = _mm3(inv, x[:C] + _mm(m2, vh))
        oh_ref[:, sl] = x[C:] + _mm(n2, vh) - _mm(n1, sa)
        upd = _mm_tn(jnp.concatenate([vh, -sa], axis=0), jnp.concatenate([ke[:, sl], be[:, sl]], axis=0))
        s_ref[0, h] = s_prev * s_decay[:, sl] + upd

    o = oh_ref[...]
    inv_n = 1.0 / HEAD_DIM
    mean = _mm_exact_rhs(o, seg) * inv_n
    oc = o - mean
    var = _mm_exact_rhs(oc * oc, seg) * inv_n
    on = oc * lax.rsqrt(var + RWKV_GN_EPS) * gnw_ref[...] + gnb_ref[...]
    bonus = _mm_exact_rhs(r * kp * rk_ref[...], seg) * v
    o_ref[...] = (on + bonus) * g


def _rwkv_call(cols, shift_prev, wkv_prev, bsz, seqlen, chunk, prm, seg):
    heads = wkv_prev.shape[1]
    aw = heads * HEAD_DIM
    n_lora = DECAY_LORA + AAA_LORA + GATE_LORA
    lora_blk = (cols.shape[1] - n_lora) // n_lora
    nc = seqlen // chunk
    sp_rkv = shift_prev[:, None, :3 * aw]
    sp_lora = shift_prev[:, None, 3 * aw:]
    vec = lambda width: pl.BlockSpec((1, width), lambda b, c: (0, 0))
    full = lambda shape: pl.BlockSpec(shape, lambda b, c: (0,) * len(shape))
    per_b3 = lambda width: pl.BlockSpec((1, 1, width), lambda b, c: (b, 0, 0))
    state_spec = pl.BlockSpec((1, heads, HEAD_DIM, HEAD_DIM), lambda b, c: (b, 0, 0, 0))
    kern = functools.partial(_rwkv_kernel, chunk=chunk, heads=heads)
    o, s_new, so_rkv, so_lora = pl.pallas_call(
        kern,
        out_shape=(jax.ShapeDtypeStruct((bsz * seqlen, aw), F32),
                   jax.ShapeDtypeStruct(wkv_prev.shape, F32),
                   jax.ShapeDtypeStruct((bsz, 1, 3 * aw), F32),
                   jax.ShapeDtypeStruct((bsz, 1, n_lora), F32)),
        grid=(bsz, nc),
        in_specs=[pl.BlockSpec((chunk, 3 * aw), lambda b, c: (b * nc + c, 0)),
                  pl.BlockSpec((chunk, n_lora), lambda b, c: (b * nc + c, lora_blk)),
                  per_b3(3 * aw), per_b3(n_lora), state_spec,
                  vec(3 * aw), vec(n_lora), vec(aw), full((DECAY_LORA, aw)), vec(aw), full((AAA_LORA, aw)),
                  full((GATE_LORA, aw)), vec(aw), vec(aw), vec(aw), vec(aw), vec(aw), full((aw, aw))],
        out_specs=(pl.BlockSpec((chunk, aw), lambda b, c: (b * nc + c, 0)), state_spec,
                   per_b3(3 * aw), per_b3(n_lora)),
        scratch_shapes=[pltpu.VMEM((chunk, aw), F32)],
        compiler_params=pltpu.CompilerParams(dimension_semantics=("parallel", "arbitrary"),
                                             vmem_limit_bytes=VMEM_LIMIT),
        name="rwkv7_chunked",
    )(cols, cols, sp_rkv, sp_lora, wkv_prev, prm["mu_rkv"], prm["mu_lora"], prm["w0"], prm["w_up"], prm["a0"],
      prm["a_up"], prm["g_up"], prm["k_k"], prm["k_a"], prm["r_k"], prm["gn_w"], prm["gn_b"], seg)
    shift = jnp.concatenate([so_rkv[:, 0], so_lora[:, 0]], axis=-1)
    return o, s_new, shift


def _gmlp_kernel(u_ref, v_ref, lng_ref, lnb_ref, ws_ref, bs_ref, o_ref, vn_ref, *, groups):
    vf = v_ref[...]
    mean = jnp.mean(vf, axis=-1, keepdims=True)
    vc = vf - mean
    var = jnp.mean(vc * vc, axis=-1, keepdims=True)
    vn = vc * lax.rsqrt(var + NORM_EPS) * lng_ref[...] + lnb_ref[...]
    vn_ref[...] = vn
    t = vf.shape[0]
    gw = vf.shape[1] // groups
    ri, ci = _iota2((t, t))
    causal = ri >= ci
    u = u_ref[...]
    for gi in range(groups):
        sl = slice(gi * gw, (gi + 1) * gw)
        w = jnp.where(causal, ws_ref[gi], 0.0)
        mixed = _mm(w, vn[:, sl]) + bs_ref[:, gi:gi + 1]
        o_ref[:, sl] = u[:, sl] * mixed


def _gmlp_call(cols, u_blk, v_blk, rows, t_chunk, ln_g, ln_b, w_s, b_s_t):
    groups = w_s.shape[0]
    bw = ln_g.shape[1]
    kern = functools.partial(_gmlp_kernel, groups=groups)
    return pl.pallas_call(
        kern,
        out_shape=(jax.ShapeDtypeStruct((rows, bw), F32), jax.ShapeDtypeStruct((rows, bw), F32)),
        grid=(rows // t_chunk,),
        in_specs=[pl.BlockSpec((t_chunk, bw), lambda i: (i, u_blk)),
                  pl.BlockSpec((t_chunk, bw), lambda i: (i, v_blk)),
                  pl.BlockSpec((1, bw), lambda i: (0, 0)), pl.BlockSpec((1, bw), lambda i: (0, 0)),
                  pl.BlockSpec((groups, t_chunk, t_chunk), lambda i: (0, 0, 0)),
                  pl.BlockSpec((t_chunk, groups), lambda i: (0, 0))],
        out_specs=(pl.BlockSpec((t_chunk, bw), lambda i: (i, 0)), pl.BlockSpec((t_chunk, bw), lambda i: (i, 0))),
        compiler_params=pltpu.CompilerParams(dimension_semantics=("parallel",),
                                             vmem_limit_bytes=VMEM_LIMIT),
        name="gmlp_spatial",
    )(cols, cols, ln_g, ln_b, w_s, b_s_t)


def _gdn_kernel(qkv_ref, z_ref, small_ref, convp_ref, s0_ref, convw_ref, alog_ref, dtb_ref, ong_ref, seg_ref,
                o_ref, s_ref, convo_ref, carry_ref, oh_ref, *, chunk, heads):
    C = chunk
    cw = heads * HEAD_DIM

    @pl.when(pl.program_id(1) == 0)
    def _():
        s_ref[...] = s0_ref[...]
        carry_ref[...] = convp_ref[0]

    x = qkv_ref[...]
    xp = jnp.concatenate([carry_ref[...], x], axis=0)
    conv = convw_ref[CONV_W - 1:CONV_W, :] * x
    for j in range(CONV_W - 1):
        off = CARRY_ROWS - (CONV_W - 1) + j
        conv = conv + convw_ref[j:j + 1, :] * xp[off:off + C, :]
    carry_ref[...] = xp[C:C + CARRY_ROWS, :]
    convo_ref[0] = x[C - (CONV_W - 1):C, :]

    act = _silu(conv)
    q = act[:, :cw]
    k = act[:, cw:2 * cw]
    v = act[:, 2 * cw:]
    seg = seg_ref[...]
    q = q * lax.rsqrt(_mm_exact_rhs(q * q, seg) + NORM_EPS) * (HEAD_DIM ** -0.5)
    k = k * lax.rsqrt(_mm_exact_rhs(k * k, seg) + NORM_EPS)

    small = small_ref[...]
    beta = _sigmoid(small)
    glog = -jnp.exp(alog_ref[...]) * _softplus(small + dtb_ref[...])
    ri, ci = _iota2((C, C))
    strict = ri > ci
    incl = ri >= ci
    eye = jnp.where(ri == ci, 1.0, 0.0)
    gc = _mm_exact_lhs(_ones_where(incl), glog)
    gc_t = _transpose_pad(gc)
    e_gc = jnp.exp(gc)
    e_end = jnp.exp(gc[C - 1:C, :] - gc)
    zs = _silu(z_ref[...])

    for h in range(heads):
        sl = slice(h * HEAD_DIM, (h + 1) * HEAD_DIM)
        gl = heads + h
        gcol = gc[:, gl:gl + 1]
        grow = gc_t[gl:gl + 1, :C]
        diff = gcol - grow
        dec_strict = jnp.exp(jnp.where(strict, diff, NEG_INF))
        dec_incl = jnp.exp(jnp.where(incl, diff, NEG_INF))
        bh = beta[:, h:h + 1]
        kh = k[:, sl]
        qh = q[:, sl]
        vh = v[:, sl]
        kq = _mm_nt(jnp.concatenate([kh, qh], axis=0), kh)
        a_mat = bh * kq[:C] * dec_strict
        attn = kq[C:] * dec_incl
        inv = _unit_lower_inv(a_mat, eye)
        eg = e_gc[:, gl:gl + 1]
        w = _mm3(inv, jnp.concatenate([bh * vh, (bh * eg) * kh], axis=1))
        s_prev = s_ref[0, h]
        u = w[:, :HEAD_DIM] - _mm(w[:, HEAD_DIM:], s_prev)
        o = _mm(qh * eg, s_prev) + _mm(attn, u)
        s_ref[0, h] = e_gc[C - 1:C, gl:gl + 1] * s_prev + _mm_tn(kh * e_end[:, gl:gl + 1], u)
        oh_ref[:, sl] = o

    o = oh_ref[...]
    ms = _mm_exact_rhs(o * o, seg) * (1.0 / HEAD_DIM)
    o_ref[...] = o * lax.rsqrt(ms + NORM_EPS) * ong_ref[...] * zs


def _gdn_call(cols, conv_prev, ssm_prev, bsz, seqlen, chunk, prm, seg):
    heads = ssm_prev.shape[1]
    cw = heads * HEAD_DIM
    nc = seqlen // chunk
    z_blk = 3
    small_blk = (5 * cw + 2 * cw) // LANES
    convp = jnp.concatenate([jnp.zeros((bsz, CARRY_ROWS - (CONV_W - 1), 3 * cw), F32), conv_prev], axis=1)
    full = lambda shape: pl.BlockSpec(shape, lambda b, c: (0,) * len(shape))
    state_spec = pl.BlockSpec((1, heads, HEAD_DIM, HEAD_DIM), lambda b, c: (b, 0, 0, 0))
    kern = functools.partial(_gdn_kernel, chunk=chunk, heads=heads)
    return pl.pallas_call(
        kern,
        out_shape=(jax.ShapeDtypeStruct((bsz * seqlen, cw), F32),
                   jax.ShapeDtypeStruct(ssm_prev.shape, F32),
                   jax.ShapeDtypeStruct((bsz, CONV_W - 1, 3 * cw), F32)),
        grid=(bsz, nc),
        in_specs=[pl.BlockSpec((chunk, 3 * cw), lambda b, c: (b * nc + c, 0)),
                  pl.BlockSpec((chunk, cw), lambda b, c: (b * nc + c, z_blk)),
                  pl.BlockSpec((chunk, LANES), lambda b, c: (b * nc + c, small_blk)),
                  pl.BlockSpec((1, CARRY_ROWS, 3 * cw), lambda b, c: (b, 0, 0)), state_spec,
                  full((CONV_W, 3 * cw)), full((1, LANES)), full((1, LANES)), full((1, cw)), full((cw, cw))],
        out_specs=(pl.BlockSpec((chunk, cw), lambda b, c: (b * nc + c, 0)), state_spec,
                   pl.BlockSpec((1, CONV_W - 1, 3 * cw), lambda b, c: (b, 0, 0))),
        scratch_shapes=[pltpu.VMEM((CARRY_ROWS, 3 * cw), F32), pltpu.VMEM((chunk, cw), F32)],
        compiler_params=pltpu.CompilerParams(dimension_semantics=("parallel", "arbitrary"),
                                             vmem_limit_bytes=VMEM_LIMIT),
        name="gdn_chunked",
    )(cols, cols, cols, convp, ssm_prev, prm["conv_w"], prm["a_log"], prm["dt_bias"], prm["onorm_g"], seg)


def _fgate_kernel(small_ref, fb_ref, lf_ref, fcol_ref, frow_ref, carry_ref, *, heads):
    tb = small_ref.shape[0]

    @pl.when(pl.program_id(1) == 0)
    def _():
        carry_ref[...] = jnp.zeros_like(carry_ref)

    lane = lax.broadcasted_iota(jnp.int32, (tb, LANES), 1)
    valid = (lane >= 2 * heads) & (lane < 3 * heads)
    lf = jnp.where(valid, -_softplus(-(small_ref[...] + fb_ref[...])), 0.0)
    lf_ref[...] = lf[:, 2 * heads:3 * heads]
    ri, ci = _iota2((tb, tb))
    fcol = _mm_exact_lhs(_ones_where(ri >= ci), lf) + carry_ref[...]
    fcol_ref[...] = fcol
    fcol_t = fcol.T if tb >= LANES else _transpose_pad(fcol)
    frow_ref[0] = fcol_t[2 * heads:3 * heads, :tb]
    carry_ref[...] = fcol[tb - 1:tb, :]


def _fgate_call(cols, small_blk, f_bias_pad, bsz, seqlen, heads, tb):
    nb = seqlen // tb
    kern = functools.partial(_fgate_kernel, heads=heads)
    return pl.pallas_call(
        kern,
        out_shape=(jax.ShapeDtypeStruct((bsz * seqlen, heads), F32),
                   jax.ShapeDtypeStruct((bsz * seqlen, LANES), F32),
                   jax.ShapeDtypeStruct((bsz, heads, seqlen), F32)),
        grid=(bsz, nb),
        in_specs=[pl.BlockSpec((tb, LANES), lambda b, i: (b * nb + i, small_blk)),
                  pl.BlockSpec((1, LANES), lambda b, i: (0, 0))],
        out_specs=(pl.BlockSpec((tb, heads), lambda b, i: (b * nb + i, 0)),
                   pl.BlockSpec((tb, LANES), lambda b, i: (b * nb + i, 0)),
                   pl.BlockSpec((1, heads, tb), lambda b, i: (b, 0, i))),
        scratch_shapes=[pltpu.VMEM((1, LANES), F32)],
        compiler_params=pltpu.CompilerParams(dimension_semantics=("parallel", "arbitrary"),
                                             vmem_limit_bytes=VMEM_LIMIT),
        name="fox_forget_cumsum",
    )(cols, f_bias_pad)


def _fox_flash_kernel(q_ref, k_ref, v_ref, fq_ref, fk_ref, o_ref, m_ref, l_ref, acc_ref, *, heads):
    i = pl.program_id(2)
    j = pl.program_id(3)
    hp = pl.program_id(1)
    tq = q_ref.shape[0]
    tk = k_ref.shape[0]

    @pl.when(j == 0)
    def _():
        m_ref[...] = jnp.full_like(m_ref, NEG_INF)
        l_ref[...] = jnp.zeros_like(l_ref)
        acc_ref[...] = jnp.zeros_like(acc_ref)

    def step(masked):
        q = q_ref[...] * (HEAD_DIM ** -0.5)
        kb = k_ref[...].astype(BF16)
        vb = v_ref[...].astype(BF16)
        lane = lax.broadcasted_iota(jnp.int32, (tq, LANES), 1)
        fq = fq_ref[...]
        fk = fk_ref[0]
        lane_h = lax.broadcasted_iota(jnp.int32, (1, LANES), 1)
        hrow = lax.broadcasted_iota(jnp.int32, (heads, 1), 0)
        if masked:
            ri, ci = _iota2((tq, tk))
            keep = ri >= ci
        for hh in range(2):
            in_head = (lane >= hh * HEAD_DIM) & (lane < (hh + 1) * HEAD_DIM)
            qh = jnp.where(in_head, q, 0.0).astype(BF16)
            s = lax.dot_general(qh, kb, (((1,), (1,)), ((), ())), preferred_element_type=F32)
            head = 2 * hp + hh
            fq_col = jnp.sum(jnp.where(lane_h == 2 * heads + head, fq, 0.0), axis=1, keepdims=True)
            fk_row = jnp.sum(jnp.where(hrow == head, fk, 0.0), axis=0, keepdims=True)
            s = s + (fq_col - fk_row)
            if masked:
                s = jnp.where(keep, s, NEG_INF)
            m_prev = m_ref[hh]
            m_new = jnp.maximum(m_prev, jnp.max(s, axis=1, keepdims=True))
            alpha = jnp.exp(m_prev - m_new)
            p = jnp.exp(s - m_new)
            l_ref[hh] = alpha * l_ref[hh] + jnp.sum(p, axis=1, keepdims=True)
            acc_ref[hh] = alpha * acc_ref[hh] + jnp.dot(p.astype(BF16), vb, preferred_element_type=F32)
            m_ref[hh] = m_new

    @pl.when(j < i)
    def _():
        step(False)

    @pl.when(j == i)
    def _():
        step(True)
        lane = lax.broadcasted_iota(jnp.int32, (tq, LANES), 1)
        o0 = acc_ref[0] / l_ref[0]
        o1 = acc_ref[1] / l_ref[1]
        o_ref[...] = jnp.where(lane < HEAD_DIM, o0, o1)


def _fox_flash_call(cols, q_blk, k_blk, v_blk, fcol, frow, bsz, seqlen, heads, tq):
    nq = seqlen // tq
    pairs = heads // 2
    kern = functools.partial(_fox_flash_kernel, heads=heads)
    kv_map = lambda off: (lambda b, hp, i, j: (b * nq + jnp.minimum(i, j), off + hp))
    return pl.pallas_call(
        kern,
        out_shape=jax.ShapeDtypeStruct((bsz * seqlen, heads * HEAD_DIM), F32),
        grid=(bsz, pairs, nq, nq),
        in_specs=[pl.BlockSpec((tq, LANES), lambda b, hp, i, j: (b * nq + i, q_blk + hp)),
                  pl.BlockSpec((tq, LANES), kv_map(k_blk)),
                  pl.BlockSpec((tq, LANES), kv_map(v_blk)),
                  pl.BlockSpec((tq, LANES), lambda b, hp, i, j: (b * nq + i, 0)),
                  pl.BlockSpec((1, heads, tq), lambda b, hp, i, j: (b, 0, jnp.minimum(i, j)))],
        out_specs=pl.BlockSpec((tq, LANES), lambda b, hp, i, j: (b * nq + i, hp)),
        scratch_shapes=[pltpu.VMEM((2, tq, 1), F32), pltpu.VMEM((2, tq, 1), F32), pltpu.VMEM((2, tq, LANES), F32)],
        compiler_params=pltpu.CompilerParams(
            dimension_semantics=("parallel", "parallel", "parallel", "arbitrary"),
            vmem_limit_bytes=VMEM_LIMIT),
        name="fox_flash",
    )(cols, cols, cols, fcol, frow)


def _fox_decode_kernel(pt_ref, q_ref, kn_ref, vn_ref, lfn_ref, *rest, heads, npp):
    del pt_ref
    k_refs = rest[:npp]
    v_refs = rest[npp:2 * npp]
    lf_refs = rest[2 * npp:3 * npp]
    o_ref, qh_ref, m_ref, l_ref, acc_ref, carry_ref = rest[3 * npp:]
    step = pl.program_id(1)
    t_new = q_ref.shape[0]
    page = k_refs[0].shape[0]

    @pl.when(step == 0)
    def _():
        q = q_ref[...] * (HEAD_DIM ** -0.5)
        for h in range(heads):
            qh_ref[h] = q[:, h * HEAD_DIM:(h + 1) * HEAD_DIM].astype(BF16)
        m_ref[...] = jnp.full_like(m_ref, NEG_INF)
        l_ref[...] = jnp.zeros_like(l_ref)
        acc_ref[...] = jnp.zeros_like(acc_ref)
        carry_ref[...] = jnp.zeros_like(carry_ref)

    def attend(h, kh, vh, bias):
        s = _mm_nt(qh_ref[h], kh) + bias
        m_prev = m_ref[h]
        m_new = jnp.maximum(m_prev, jnp.max(s, axis=1, keepdims=True))
        alpha = jnp.exp(m_prev - m_new)
        p = jnp.exp(s - m_new)
        l_ref[h] = alpha * l_ref[h] + jnp.sum(p, axis=1, keepdims=True)
        acc_ref[h] = alpha * acc_ref[h] + _mm(p, vh)
        m_ref[h] = m_new

    ji, si = _iota2((page, page))
    newer = _ones_where(ji > si)
    for u in range(npp - 1, -1, -1):
        lf = lf_refs[u][...]
        lf_pad = jnp.concatenate([lf, jnp.zeros((page, LANES - heads), F32)], axis=1)
        lf_t = _transpose_pad(lf_pad)[:heads, :]
        suffix = _mm_exact_rhs(lf_t, newer) + carry_ref[...]
        carry_ref[...] = carry_ref[...] + jnp.sum(lf_t, axis=1, keepdims=True)
        for h in range(heads):
            attend(h, k_refs[u][:, h, :], v_refs[u][:, h, :], suffix[h:h + 1, :])

    @pl.when(step == pl.num_programs(1) - 1)
    def _():
        lfn_t = _transpose_pad(lfn_ref[...])[:heads, :]
        ji2, si2 = _iota2((LANES, LANES))
        cum_new = _mm_exact_rhs(lfn_t, _ones_where(ji2 <= si2))
        ri, ci = _iota2((t_new, t_new))
        kn = kn_ref[...]
        vn = vn_ref[...]
        for h in range(heads):
            sl = slice(h * HEAD_DIM, (h + 1) * HEAD_DIM)
            bias = jnp.where(ci <= ri, -cum_new[h:h + 1, :t_new], NEG_INF)
            attend(h, kn[:, sl], vn[:, sl], bias)
            o_ref[:, sl] = acc_ref[h] / l_ref[h]


def _fox_decode_call(cols, q_blk, k_blk, v_blk, lf_new, pool_k, pool_v, pool_lf, layer, page_table, heads, npp):
    bsz, n_pages = page_table.shape
    t_new = cols.shape[0] // bsz
    width = heads * HEAD_DIM
    page = pool_k.shape[2]
    steps = n_pages // npp

    def page_map(u, nd):
        return lambda b, s, pt: (layer, pt[b, (steps - 1 - s) * npp + u]) + (0,) * nd

    k_specs = [pl.BlockSpec((None, None, page, heads, HEAD_DIM), page_map(u, 3)) for u in range(npp)]
    v_specs = [pl.BlockSpec((None, None, page, heads, HEAD_DIM), page_map(u, 3)) for u in range(npp)]
    lf_specs = [pl.BlockSpec((None, None, page, heads), page_map(u, 2)) for u in range(npp)]
    row_spec = lambda blk: pl.BlockSpec((t_new, width), lambda b, s, pt: (b, blk))
    kern = functools.partial(_fox_decode_kernel, heads=heads, npp=npp)
    grid_spec = pltpu.PrefetchScalarGridSpec(
        num_scalar_prefetch=1,
        grid=(bsz, steps),
        in_specs=[row_spec(q_blk), row_spec(k_blk), row_spec(v_blk),
                  pl.BlockSpec((t_new, LANES), lambda b, s, pt: (b, 0))] + k_specs + v_specs + lf_specs,
        out_specs=pl.BlockSpec((t_new, width), lambda b, s, pt: (b, 0)),
        scratch_shapes=[pltpu.VMEM((heads, t_new, HEAD_DIM), BF16), pltpu.VMEM((heads, t_new, 1), F32),
                        pltpu.VMEM((heads, t_new, 1), F32), pltpu.VMEM((heads, t_new, HEAD_DIM), F32),
                        pltpu.VMEM((heads, 1), F32)],
    )
    return pl.pallas_call(
        kern,
        out_shape=jax.ShapeDtypeStruct((bsz * t_new, width), F32),
        grid_spec=grid_spec,
        compiler_params=pltpu.CompilerParams(dimension_semantics=("parallel", "arbitrary"),
                                             vmem_limit_bytes=VMEM_LIMIT),
        name="fox_paged_decode",
    )(page_table, cols, cols, cols, lf_new, *([pool_k] * npp), *([pool_v] * npp), *([pool_lf] * npp))


def _block_ones(width):
    idx = jnp.arange(width) // HEAD_DIM
    return (idx[:, None] == idx[None, :]).astype(BF16)


def _pick_tile(n, cap):
    t = min(n, cap)
    while n % t:
        t //= 2
    return t


def _even_weights(w_in, n_a_cols, b_width):
    n_lora = DECAY_LORA + AAA_LORA + GATE_LORA
    rkv = n_a_cols - n_lora
    return jnp.concatenate([w_in[:, :rkv], w_in[:, n_a_cols:], w_in[:, rkv:n_a_cols]], axis=1).astype(BF16)


def _odd_weights(w_in, cw, heads_c, dw, heads_d):
    d = w_in.shape[0]
    c0 = 4 * cw
    d0 = c0 + 2 * heads_c
    small = jnp.concatenate([w_in[:, c0:d0], w_in[:, d0 + 3 * dw:d0 + 3 * dw + heads_d]], axis=1)
    n_main = 4 * cw + 3 * dw
    n_tot = -(-(n_main + LANES) // (3 * LANES)) * (3 * LANES)
    pad = jnp.zeros((d, n_tot - n_main - small.shape[1]), w_in.dtype)
    return jnp.concatenate([w_in[:, :c0], w_in[:, d0:d0 + 3 * dw], small, pad], axis=1).astype(BF16)


def _pad_lanes(vec, offset):
    out = jnp.zeros((1, LANES), F32)
    return out.at[0, offset:offset + vec.shape[0]].set(vec.astype(F32))


def _run_group(x, mods, is_prompt, st, prm):
    bsz, seqlen, d = x.shape
    rows = bsz * seqlen
    x2 = x.reshape(rows, d)
    depth = prm["depth"]
    if is_prompt:
        tm = _pick_tile(seqlen, 512)
        tiles_per_mod = seqlen // tm
        expand = lambda m: m[:, None, :]
    else:
        tm = rows
        tiles_per_mod = 1
        expand = lambda m: jnp.repeat(m, seqlen, axis=0)[None]
    outs = {k: [] for k in ("wkv", "shift", "bv", "ssm", "conv", "dk", "dv", "dlf")}
    for l in range(depth):
        i = l // 2
        sh1, sc1, gt1, sh2, sc2, gt2 = (expand(m) for m in jnp.split(mods[l], 6, axis=-1))
        g = prm["norm_g"][l]
        if l % 2 == 0:
            e = prm["even"][i]
            heads = e["heads"]
            aw = heads * HEAD_DIM
            cols = _inproj_call(x2, sc1, sh1, g[0:1], e["w_in"], tm, e["w_in"].shape[1] // 2, tiles_per_mod)
            chunk = min(GDN_CHUNK, seqlen)
            if is_prompt:
                shift_prev = jnp.zeros((bsz, e["n_a_cols"]), F32)
                wkv_prev = jnp.zeros((bsz, heads, HEAD_DIM, HEAD_DIM), F32)
            else:
                shift_prev, wkv_prev = st["a_shift"][i], st["a_wkv"][i]
            o_a, wkv, shift = _rwkv_call(cols, shift_prev, wkv_prev, bsz, seqlen, chunk, e, prm["seg"])
            bw = e["b_width"]
            t_chunk = min(B_CHUNK, seqlen)
            o_b, v_rows = _gmlp_call(cols, 3 * aw // bw, 3 * aw // bw + 1, rows, t_chunk, e["ln_g"], e["ln_b"],
                                     e["w_s"][:, :t_chunk, :t_chunk], e["b_s_t"][:t_chunk])
            outs["wkv"].append(wkv)
            outs["shift"].append(shift)
            outs["bv"].append(v_rows.reshape(bsz, seqlen, bw))
            w_out = e["w_out"]
            split = aw
        else:
            o = prm["odd"][i]
            heads = o["heads"]
            cw = heads * HEAD_DIM
            cols = _inproj_call(x2, sc1, sh1, g[0:1], o["w_in"], tm, o["w_in"].shape[1] // 3, tiles_per_mod)
            chunk = min(GDN_CHUNK, seqlen)
            if is_prompt:
                conv_prev = jnp.zeros((bsz, CONV_W - 1, 3 * cw), F32)
                ssm_prev = jnp.zeros((bsz, heads, HEAD_DIM, HEAD_DIM), F32)
            else:
                conv_prev, ssm_prev = st["c_conv"][i], st["c_ssm"][i]
            o_a, ssm, conv = _gdn_call(cols, conv_prev, ssm_prev, bsz, seqlen, chunk, o, prm["seg"])
            q_blk, k_blk, v_blk = (4 * cw) // LANES, (5 * cw) // LANES, (6 * cw) // LANES
            small_blk = (7 * cw) // LANES
            tb = _pick_tile(seqlen, 512)
            logf, fcol, frow = _fgate_call(cols, small_blk, o["f_bias"], bsz, seqlen, heads, tb)
            if is_prompt:
                o_b = _fox_flash_call(cols, q_blk, k_blk, v_blk, fcol, frow, bsz, seqlen, heads, tb)
            else:
                lf_new = jnp.pad(logf, ((0, 0), (0, LANES - heads)))
                o_b = _fox_decode_call(cols, q_blk * LANES // cw, k_blk * LANES // cw, v_blk * LANES // cw, lf_new, st["d_k"], st["d_v"], st["d_logf"], i,
                                       st["page_table"], heads, prm["npp"])
            outs["ssm"].append(ssm)
            outs["conv"].append(conv)
            outs["dk"].append(cols[:, 5 * cw:6 * cw].reshape(bsz, seqlen, heads, HEAD_DIM))
            outs["dv"].append(cols[:, 6 * cw:7 * cw].reshape(bsz, seqlen, heads, HEAD_DIM))
            outs["dlf"].append(logf.reshape(bsz, seqlen, heads))
            w_out = o["w_out"]
            split = cw
        x2 = _outproj_call(o_a, o_b, x2, gt1, g[1:2], w_out[:split], w_out[split:], tm, tiles_per_mod)
        x2 = _mlp_call(x2, sc2, sh2, gt2, g[2:3], g[3:4], prm["w_up"][l], prm["w_down"][l], tm,
                       _pick_tile(prm["w_up"][l].shape[1], 1024), tiles_per_mod)
    return x2.reshape(bsz, seqlen, d), {k: jnp.stack(v) for k, v in outs.items()}


def kernel(x_prompt, x_sample, state_a_wkv, state_a_shift, state_c_ssm, state_c_conv, cache_d_k, cache_d_v,
           cache_d_logf, page_table, c_prompt, c_sample, w_ada, b_ada, norm_g, w_in_even, w_out_even, a_mu, a_w0,
           a_w_up, a_a0, a_a_up, a_g_up, a_k_k, a_k_a, a_r_k, a_gn_w, a_gn_b, b_ln_g, b_ln_b, b_ws, b_bs,
           w_in_odd, w_out_odd, c_conv_w, c_a_log, c_dt_bias, c_onorm_g, d_f_bias, w_up, w_down):
    depth = w_ada.shape[0]
    bp = x_prompt.shape[0]
    n_a_cols = a_mu.shape[1]
    aw = a_w0.shape[1]
    n_lora = DECAY_LORA + AAA_LORA + GATE_LORA
    b_width = b_ln_g.shape[1]
    heads_a = aw // HEAD_DIM
    heads_c = c_a_log.shape[1]
    cw = heads_c * HEAD_DIM
    heads_d = d_f_bias.shape[1]
    dw = heads_d * HEAD_DIM
    assert heads_a == heads_c == heads_d and cw == dw == aw == b_width

    mods = _ada_call(jnp.concatenate([c_prompt, c_sample], axis=0), w_ada, b_ada)

    row = lambda vec: vec.reshape(1, -1).astype(F32)
    even = []
    for i in range(w_in_even.shape[0]):
        even.append(dict(
            heads=heads_a, n_a_cols=n_a_cols, b_width=b_width,
            w_in=_even_weights(w_in_even[i], n_a_cols, b_width), w_out=w_out_even[i].astype(BF16),
            mu_rkv=row(a_mu[i, :n_a_cols - n_lora]), mu_lora=row(a_mu[i, n_a_cols - n_lora:]),
            w0=row(a_w0[i]), w_up=a_w_up[i], a0=row(a_a0[i]), a_up=a_a_up[i], g_up=a_g_up[i],
            k_k=row(a_k_k[i]), k_a=row(a_k_a[i]), r_k=row(a_r_k[i]), gn_w=row(a_gn_w[i]), gn_b=row(a_gn_b[i]),
            ln_g=row(b_ln_g[i]), ln_b=row(b_ln_b[i]), w_s=b_ws[i], b_s_t=b_bs[i].T))
    odd = []
    for i in range(w_in_odd.shape[0]):
        odd.append(dict(
            heads=heads_c,
            w_in=_odd_weights(w_in_odd[i], cw, heads_c, dw, heads_d), w_out=w_out_odd[i].astype(BF16),
            conv_w=c_conv_w[i], a_log=_pad_lanes(c_a_log[i], heads_c), dt_bias=_pad_lanes(c_dt_bias[i], heads_c),
            onorm_g=jnp.tile(c_onorm_g[i], heads_c).reshape(1, cw), f_bias=_pad_lanes(d_f_bias[i], 2 * heads_c)))
    prm = dict(depth=depth, norm_g=norm_g, even=even, odd=odd, seg=_block_ones(aw), npp=min(8, page_table.shape[1]),
               w_up=[w_up[l].astype(BF16) for l in range(depth)],
               w_down=[w_down[l].astype(BF16) for l in range(depth)])

    y_p, op = _run_group(x_prompt, mods[:, :bp], True, None, prm)
    st = dict(a_wkv=state_a_wkv, a_shift=state_a_shift, c_ssm=state_c_ssm, c_conv=state_c_conv,
              d_k=cache_d_k, d_v=cache_d_v, d_logf=cache_d_logf, page_table=page_table)
    y_s, os_ = _run_group(x_sample, mods[:, bp:], False, st, prm)
    return (y_p, y_s, op["wkv"], op["shift"], op["ssm"], op["conv"], op["dk"], op["dv"], op["dlf"],
            os_["wkv"], os_["shift"], os_["bv"], os_["ssm"], os_["conv"], os_["dk"], os_["dv"], os_["dlf"])
```
